```python
import jax, jax.numpy as jnp
from jax import lax
import numpy as np

D_MODEL = 1024
BATCH = 16
SEQ = 2048
DEPTH = 1
DEC_BATCH = 8
DEC_SEQ = 16
PAST_LEN = 2048

CHUNK = 64
LEFT_CHUNKS = 8
LEFT_REACH = LEFT_CHUNKS * CHUNK
BAND = LEFT_REACH + CHUNK
MAX_REL = 128
ATTN_HEADS = 8
HEAD_DIM = D_MODEL // 16
ATTN_WIDTH = ATTN_HEADS * HEAD_DIM
ATTN_SCALE = HEAD_DIM ** -0.5
RET_HEADS = 8
RET_DK = D_MODEL // 16
RET_DV = D_MODEL // 16
RET_WIDTH = RET_HEADS * RET_DV
RET_CHUNK = CHUNK
ROPE_BASE = 10000.0
MIX_WIDTH = ATTN_WIDTH + RET_WIDTH
IN_WIDTHS = (ATTN_WIDTH, ATTN_WIDTH, ATTN_WIDTH, RET_HEADS * RET_DK, RET_HEADS * RET_DK, RET_WIDTH, RET_WIDTH)
IN_COLS = sum(IN_WIDTHS)
N_EXPERTS = 32
TOP_K = 4
D_FF = D_MODEL
SWIGLU_ALPHA = 1.702
SWIGLU_LIMIT = 7.0
EXPERT_BLOCK = 256
PLE_DIM = 256
EPS = 1e-6

kernel_name = 'hybrid_chunkattn_retention_moe_stream_step'


def rmsnorm(x, g):
    xf = x.astype(jnp.float32)
    y = xf * lax.rsqrt(jnp.mean(xf * xf, axis=-1, keepdims=True) + EPS)
    return (y * g.astype(jnp.float32)).astype(x.dtype)


def rotary(x, pos):
    half = x.shape[-1] // 2
    freq = ROPE_BASE ** (-jnp.linspace(0.0, 1.0, half, dtype=jnp.float32))
    ang = pos.astype(jnp.float32)[:, None] * freq[None, :]
    cos = jnp.cos(ang)[None, :, None, :]
    sin = jnp.sin(ang)[None, :, None, :]
    xf = x.astype(jnp.float32)
    x1, x2 = xf[..., :half], xf[..., half:]
    return jnp.concatenate([x1 * cos - x2 * sin, x1 * sin + x2 * cos], axis=-1).astype(x.dtype)


def project_in(h, g_mix, w_in, pos):
    b, s, _ = h.shape
    z = rmsnorm(h, g_mix) @ w_in
    qa, ka, va, qb, kb, vb, gb = jnp.split(z, np.cumsum(IN_WIDTHS)[:-1].tolist(), axis=-1)
    qa = qa.reshape(b, s, ATTN_HEADS, HEAD_DIM)
    ka = ka.reshape(b, s, ATTN_HEADS, HEAD_DIM)
    va = va.reshape(b, s, ATTN_HEADS, HEAD_DIM)
    qb = rotary(qb.reshape(b, s, RET_HEADS, RET_DK), pos)
    kb = rotary(kb.reshape(b, s, RET_HEADS, RET_DK), pos) * (RET_DK ** -0.5)
    vb = vb.reshape(b, s, RET_HEADS, RET_DV)
    return qa, ka, va, qb, kb, vb, gb


def band_attention_prompt(q, k, v, rel_bias):
    b, s, h, dh = q.shape
    nc = s // CHUNK
    kp = jnp.pad(k, ((0, 0), (LEFT_REACH, 0), (0, 0), (0, 0)))
    vp = jnp.pad(v, ((0, 0), (LEFT_REACH, 0), (0, 0), (0, 0)))
    qc = q.reshape(b, nc, CHUNK, h, dh).swapaxes(0, 1)
    dist = (jnp.arange(CHUNK)[:, None] + LEFT_REACH) - jnp.arange(BAND)[None, :]
    bias = rel_bias[:, jnp.clip(dist, -MAX_REL, MAX_REL) + MAX_REL].astype(jnp.float32)

    def one_chunk(args):
        c, q_blk = args
        start = c * CHUNK
        k_blk = lax.dynamic_slice_in_dim(kp, start, BAND, axis=1)
        v_blk = lax.dynamic_slice_in_dim(vp, start, BAND, axis=1)
        k_pos = start - LEFT_REACH + jnp.arange(BAND)
        sc = jnp.einsum('bqhd,bkhd->bhqk', q_blk, k_blk).astype(jnp.float32) * ATTN_SCALE + bias
        sc = jnp.where(k_pos >= 0, sc, -jnp.inf)
        w = jax.nn.softmax(sc, axis=-1).astype(v_blk.dtype)
        return jnp.einsum('bhqk,bkhd->bqhd', w, v_blk)

    o = lax.map(one_chunk, (jnp.arange(nc), qc))
    return o.swapaxes(0, 1).reshape(b, s, h, dh)


def band_attention_sample(q, k_new, v_new, k_cache, v_cache, rel_bias, q_pos):
    n_cache = k_cache.shape[1]
    k_all = jnp.concatenate([k_cache.astype(k_new.dtype), k_new], axis=1)
    v_all = jnp.concatenate([v_cache.astype(v_new.dtype), v_new], axis=1)
    k_pos = jnp.concatenate([q_pos[0] - n_cache + jnp.arange(n_cache), q_pos])
    dist = jnp.clip(q_pos[:, None] - k_pos[None, :], -MAX_REL, MAX_REL) + MAX_REL
    bias = rel_bias[:, dist].astype(jnp.float32)
    sc = jnp.einsum('bqhd,bkhd->bhqk', q, k_all).astype(jnp.float32) * ATTN_SCALE + bias
    w = jax.nn.softmax(sc, axis=-1).astype(v_all.dtype)
    return jnp.einsum('bhqk,bkhd->bqhd', w, v_all)


def retention(q, k, v, s0):
    b, s, h, dk = q.shape
    dv = v.shape[-1]
    c = min(RET_CHUNK, s)
    nc = s // c
    log_g = jnp.log1p(-jnp.exp2(-5.0 - jnp.arange(RET_HEADS, dtype=jnp.float32)))
    idx = jnp.arange(c, dtype=jnp.float32)
    rel = idx[:, None] - idx[None, :]
    dmask = jnp.where(rel >= 0, jnp.exp(log_g[:, None, None] * jnp.maximum(rel, 0.0)), 0.0)
    qc = q.astype(jnp.float32).reshape(b, nc, c, h, dk)
    kc = k.astype(jnp.float32).reshape(b, nc, c, h, dk)
    vc = v.astype(jnp.float32).reshape(b, nc, c, h, dv)
    scores = jnp.einsum('bnihk,bnjhk->bnhij', qc, kc) * dmask
    o_inner = jnp.einsum('bnhij,bnjhv->bnihv', scores, vc)
    zeta = jnp.exp(log_g[:, None] * (c - 1.0 - idx)[None, :])
    kv = jnp.einsum('bnjhk,bnjhv,hj->nbhkv', kc, vc, zeta)
    chunk_decay = jnp.exp(log_g * c)[:, None, None]

    def step(state, kv_n):
        return chunk_decay * state + kv_n, state

    s_final, s_prev = lax.scan(step, s0.astype(jnp.float32), kv)
    xi = jnp.exp(log_g[:, None] * (idx + 1.0)[None, :])
    o_cross = jnp.einsum('bnihk,nbhkv,hi->bnihv', qc, s_prev, xi)
    return (o_inner + o_cross).reshape(b, s, h, dv), s_final


def retention_group_out(o, gb, g_ret):
    b, s = o.shape[:2]
    on = o * lax.rsqrt(jnp.mean(o * o, axis=-1, keepdims=True) + EPS)
    on = on.reshape(b, s, RET_WIDTH) * g_ret.astype(jnp.float32)
    return on.astype(gb.dtype) * jax.nn.silu(gb)


def moe(u, w_router, b_router, w_gu, b_gu, w_down, b_down):
    shape = u.shape
    x = u.reshape(-1, shape[-1])
    n_tok, d = x.shape
    logits = (x @ w_router).astype(jnp.float32) + b_router.astype(jnp.float32)
    top_val, top_idx = lax.top_k(logits, TOP_K)
    gates = jax.nn.softmax(top_val, axis=-1)
    n = n_tok * TOP_K
    flat_e = top_idx.reshape(-1)
    order = jnp.argsort(flat_e)
    sorted_e = flat_e[order]
    tok = order // TOP_K
    counts = jnp.bincount(flat_e, length=N_EXPERTS)
    padded = (counts + EXPERT_BLOCK - 1) // EXPERT_BLOCK * EXPERT_BLOCK
    pend = jnp.cumsum(padded)
    dest = (pend - padded)[sorted_e] + jnp.arange(n) - (jnp.cumsum(counts) - counts)[sorted_e]
    n_blocks = (n + N_EXPERTS * (EXPERT_BLOCK - 1) + EXPERT_BLOCK - 1) // EXPERT_BLOCK
    rows = jnp.zeros((n_blocks * EXPERT_BLOCK, d), x.dtype).at[dest].set(x[tok])
    block_expert = jnp.minimum(
        jnp.searchsorted(pend, jnp.arange(n_blocks) * EXPERT_BLOCK, side='right'), N_EXPERTS - 1)

    def expert_block(args):
        xb, e = args
        gu = xb @ w_gu[e] + b_gu[e]
        glu = jnp.minimum(gu[:, :D_FF], SWIGLU_LIMIT)
        lin = jnp.clip(gu[:, D_FF:], -SWIGLU_LIMIT, SWIGLU_LIMIT)
        act = glu * jax.nn.sigmoid(SWIGLU_ALPHA * glu) * (lin + 1.0)
        return act @ w_down[e] + b_down[e]

    out = lax.map(expert_block, (rows.reshape(n_blocks, EXPERT_BLOCK, d), block_expert))
    contrib = out.reshape(-1, d)[dest] * gates.reshape(-1)[order][:, None]
    y = jnp.zeros_like(x).at[tok].add(contrib.astype(x.dtype))
    return y.reshape(shape)


def finish_layer(h, oa, o_ret, gb, p, g_ret, w_out, g_ffn, w_router, b_router, w_gu, b_gu,
                 w_down, b_down, g_ple, w_ple_gate, w_ple_proj):
    b, s, _ = h.shape
    mixed = jnp.concatenate([oa.reshape(b, s, ATTN_WIDTH), retention_group_out(o_ret, gb, g_ret)], axis=-1)
    h = h + mixed @ w_out
    h = h + moe(rmsnorm(h, g_ffn), w_router, b_router, w_gu, b_gu, w_down, b_down)
    gate = jax.nn.sigmoid((rmsnorm(h, g_ple) @ w_ple_gate).astype(jnp.float32))
    h = h + ((p @ w_ple_proj).astype(jnp.float32) * gate).astype(h.dtype)
    return h


def setup_inputs(seed: int = 0) -> dict:
    key = jax.random.key(seed)
    ks = jax.random.split(key, 23)

    def nrm(k, shape, scale=1.0):
        return jax.random.normal(k, shape, jnp.float32) * scale

    def gain(k, shape):
        return 1.0 + nrm(k, shape, 0.05)

    a_keep = min(LEFT_REACH, PAST_LEN)
    return {
        'x_prompt': nrm(ks[0], (BATCH, SEQ, D_MODEL)),
        'x_sample': nrm(ks[1], (DEC_BATCH, DEC_SEQ, D_MODEL)),
        'cache_k_attn': nrm(ks[2], (DEPTH, DEC_BATCH, a_keep, ATTN_HEADS, HEAD_DIM)),
        'cache_v_attn': nrm(ks[3], (DEPTH, DEC_BATCH, a_keep, ATTN_HEADS, HEAD_DIM)),
        'state_ret': nrm(ks[4], (DEPTH, DEC_BATCH, RET_HEADS, RET_DK, RET_DV), 0.3),
        'p_prompt': nrm(ks[5], (DEPTH, BATCH, SEQ, PLE_DIM)),
        'p_sample': nrm(ks[6], (DEPTH, DEC_BATCH, DEC_SEQ, PLE_DIM)),
        'g_mix': gain(ks[7], (DEPTH, D_MODEL)),
        'w_in': nrm(ks[8], (DEPTH, D_MODEL, IN_COLS), D_MODEL ** -0.5),
        'rel_bias': nrm(ks[9], (DEPTH, ATTN_HEADS, 2 * MAX_REL + 1), 0.2),
        'g_ret': gain(ks[10], (DEPTH, RET_WIDTH)),
        'w_out': nrm(ks[11], (DEPTH, MIX_WIDTH, D_MODEL), MIX_WIDTH ** -0.5),
        'g_ffn': gain(ks[12], (DEPTH, D_MODEL)),
        'w_router': nrm(ks[13], (DEPTH, D_MODEL, N_EXPERTS), D_MODEL ** -0.5),
        'b_router': nrm(ks[14], (DEPTH, N_EXPERTS), 0.01),
        'w_gu': nrm(ks[15], (DEPTH, N_EXPERTS, D_MODEL, 2 * D_FF), D_MODEL ** -0.5),
        'b_gu': nrm(ks[16], (DEPTH, N_EXPERTS, 2 * D_FF), 0.01),
        'w_down': nrm(ks[17], (DEPTH, N_EXPERTS, D_FF, D_MODEL), D_FF ** -0.5),
        'b_down': nrm(ks[18], (DEPTH, N_EXPERTS, D_MODEL), 0.01),
        'g_ple': gain(ks[19], (DEPTH, D_MODEL)),
        'w_ple_gate': nrm(ks[20], (DEPTH, D_MODEL, D_MODEL), D_MODEL ** -0.5),
        'w_ple_proj': nrm(ks[21], (DEPTH, PLE_DIM, D_MODEL), PLE_DIM ** -0.5),
        'g_final': gain(ks[22], (D_MODEL,)),
    }


def reference(x_prompt, x_sample, cache_k_attn, cache_v_attn, state_ret, p_prompt, p_sample,
              g_mix, w_in, rel_bias, g_ret, w_out, g_ffn, w_router, b_router, w_gu, b_gu,
              w_down, b_down, g_ple, w_ple_gate, w_ple_proj, g_final):
    bp, sp, _ = x_prompt.shape
    bs, ss, _ = x_sample.shape
    pos_p = jnp.arange(sp)
    pos_s = PAST_LEN + jnp.arange(ss)
    keep_p = min(LEFT_REACH, sp)
    h_p, h_s = x_prompt, x_sample
    k_p_rows, v_p_rows, s_p_list = [], [], []
    k_s_rows, v_s_rows, s_s_list = [], [], []
    for i in range(DEPTH):
        qa, ka, va, qb, kb, vb, gb = project_in(h_p, g_mix[i], w_in[i], pos_p)
        oa = band_attention_prompt(qa, ka, va, rel_bias[i])
        o_ret, s_fin = retention(qb, kb, vb, jnp.zeros((bp, RET_HEADS, RET_DK, RET_DV), jnp.float32))
        h_p = finish_layer(h_p, oa, o_ret, gb, p_prompt[i], g_ret[i], w_out[i], g_ffn[i], w_router[i],
                           b_router[i], w_gu[i], b_gu[i], w_down[i], b_down[i], g_ple[i],
                           w_ple_gate[i], w_ple_proj[i])
        k_p_rows.append(ka[:, sp - keep_p:])
        v_p_rows.append(va[:, sp - keep_p:])
        s_p_list.append(s_fin)
        qa, ka, va, qb, kb, vb, gb = project_in(h_s, g_mix[i], w_in[i], pos_s)
        oa = band_attention_sample(qa, ka, va, cache_k_attn[i], cache_v_attn[i], rel_bias[i], pos_s)
        o_ret, s_fin = retention(qb, kb, vb, state_ret[i])
        h_s = finish_layer(h_s, oa, o_ret, gb, p_sample[i], g_ret[i], w_out[i], g_ffn[i], w_router[i],
                           b_router[i], w_gu[i], b_gu[i], w_down[i], b_down[i], g_ple[i],
                           w_ple_gate[i], w_ple_proj[i])
        k_s_rows.append(ka)
        v_s_rows.append(va)
        s_s_list.append(s_fin)
    y_prompt = rmsnorm(h_p, g_final)
    y_sample = rmsnorm(h_s, g_final)
    return (y_prompt, y_sample,
            jnp.stack(k_p_rows), jnp.stack(v_p_rows), jnp.stack(s_p_list),
            jnp.stack(k_s_rows), jnp.stack(v_s_rows), jnp.stack(s_s_list))
```

```python
import functools

import numpy as np
import jax
import jax.numpy as jnp
from jax import lax
from jax.experimental import pallas as pl
from jax.experimental.pallas import tpu as pltpu

F32 = jnp.float32
BF16 = jnp.bfloat16
I32 = jnp.int32

CHUNK = 64
LEFT_REACH = 8 * CHUNK
MAX_REL = 128
HEADS = 8
HEAD_DIM = 64
WIDTH = HEADS * HEAD_DIM
ATTN_SCALE = HEAD_DIM ** -0.5
RET_SCALE = HEAD_DIM ** -0.5
ROPE_BASE = 10000.0
PAST_LEN = 2048
N_EXPERTS = 32
TOP_K = 4
SWIGLU_ALPHA = 1.702
SWIGLU_LIMIT = 7.0
EPS = 1e-6

LANES = 128
PAIR = LANES // HEAD_DIM
GROUPS = WIDTH // LANES

TOKEN_TILE = 512
ATTN_QBLOCK = 256
RET_BLOCK = 256
EXPERT_ROWS = 256
VMEM_LIMIT = 56 * 1024 * 1024


def _cparams(sem):
    return pltpu.CompilerParams(dimension_semantics=sem, vmem_limit_bytes=VMEM_LIMIT)


def _rmsnorm(x, g):
    return x * lax.rsqrt(jnp.mean(x * x, axis=-1, keepdims=True) + EPS) * g


def _const_spec(shape):
    nd = len(shape)
    return pl.BlockSpec(shape, lambda *_: (0,) * nd)


def _inproj_kernel(x_ref, g_ref, w_ref, cos_ref, sin_ref,
                   qa_ref, ka_ref, va_ref, qb_ref, kb_ref, vb_ref, gb_ref, kaf_ref, vaf_ref,
                   *, tiles_per_seq):
    i = pl.program_id(0)
    u = _rmsnorm(x_ref[...], g_ref[...]).astype(BF16)

    def col(c):
        return jnp.dot(u, w_ref[:, c * WIDTH:(c + 1) * WIDTH], preferred_element_type=F32)

    qa_ref[...] = (col(0) * ATTN_SCALE).astype(BF16)
    ka = col(1)
    va = col(2)
    ka_ref[...] = ka.astype(BF16)
    va_ref[...] = va.astype(BF16)

    @pl.when(i % tiles_per_seq == tiles_per_seq - 1)
    def _():
        kaf_ref[...] = ka
        vaf_ref[...] = va

    cos = cos_ref[...]
    sin = sin_ref[...]
    lane = lax.broadcasted_iota(I32, cos.shape, 1)
    first_half = (lane & (HEAD_DIM - 1)) < HEAD_DIM // 2

    def rotary(t):
        partner = jnp.where(first_half, pltpu.roll(t, WIDTH - HEAD_DIM // 2, 1), pltpu.roll(t, HEAD_DIM // 2, 1))
        return t * cos + partner * sin

    qb_ref[...] = rotary(col(3)).astype(BF16)
    kb_ref[...] = (rotary(col(4)) * RET_SCALE).astype(BF16)
    vb_ref[...] = col(5).astype(BF16)
    gb_ref[...] = col(6).astype(BF16)


def _in_project(x2d, g_mix, w_in_bf, cos_t, sin_t, *, tm, seq, keep):
    t, d = x2d.shape
    n_tiles = t // tm
    if tm <= seq:
        assert seq % tm == 0 and keep == tm
        tiles_per_seq = seq // tm
    else:
        assert tm == t and keep == seq
        tiles_per_seq = 1
    tab_tiles = cos_t.shape[0] // tm
    n_keep_rows = (t // seq) * keep
    tile = pl.BlockSpec((tm, WIDTH), lambda i: (i, 0))
    out_shape = [jax.ShapeDtypeStruct((t, WIDTH), BF16)] * 7 + [jax.ShapeDtypeStruct((n_keep_rows, WIDTH), F32)] * 2
    keep_spec = pl.BlockSpec((tm, WIDTH), lambda i: (i // tiles_per_seq, 0))
    tab_spec = pl.BlockSpec((tm, WIDTH), lambda i: (i % tab_tiles, 0))
    return pl.pallas_call(
        functools.partial(_inproj_kernel, tiles_per_seq=tiles_per_seq),
        out_shape=out_shape,
        grid=(n_tiles,),
        in_specs=[pl.BlockSpec((tm, d), lambda i: (i, 0)), _const_spec((1, d)), _const_spec(w_in_bf.shape),
                  tab_spec, tab_spec],
        out_specs=[tile] * 7 + [keep_spec] * 2,
        compiler_params=_cparams(("arbitrary",)),
        name="in_project",
    )(x2d, g_mix, w_in_bf, cos_t, sin_t)


def _attend(q, segments):
    lane = lax.broadcasted_iota(I32, (1, LANES), 1)
    outs = []
    for g in range(GROUPS):
        sl = slice(g * LANES, (g + 1) * LANES)
        qg = q[:, sl]
        acc = None
        for par in range(PAIR):
            h = PAIR * g + par
            head_lanes = (lane // HEAD_DIM) == par
            qm = jnp.where(head_lanes, qg, jnp.zeros_like(qg))
            scores = []
            for k, _, bias, allowed in segments:
                s = lax.dot_general(qm, k[:, sl], (((1,), (1,)), ((), ())), preferred_element_type=F32) + bias(h)
                if allowed is not None:
                    s = jnp.where(allowed, s, -jnp.inf)
                scores.append(s)
            m = functools.reduce(jnp.maximum, [jnp.max(s, axis=-1, keepdims=True) for s in scores])
            probs = [jnp.exp(s - m) for s in scores]
            denom = functools.reduce(jnp.add, [jnp.sum(p, axis=-1, keepdims=True) for p in probs])
            o = functools.reduce(jnp.add, [jnp.dot(p.astype(BF16), seg[1][:, sl], preferred_element_type=F32)
                                           for p, seg in zip(probs, segments)])
            o = o / denom
            acc = o if acc is None else jnp.where(head_lanes, o, acc)
        outs.append(acc)
    return jnp.concatenate(outs, axis=1)


def _attn_prompt_kernel(q_ref, k0_ref, k1_ref, k2_ref, v0_ref, v1_ref, v2_ref, bias_ref, o_ref, *, qblk):
    j = pl.program_id(1)
    k = jnp.concatenate([k0_ref[...], k1_ref[...], k2_ref[...]], axis=0)
    v = jnp.concatenate([v0_ref[...], v1_ref[...], v2_ref[...]], axis=0)
    nk = 3 * qblk
    qi = lax.broadcasted_iota(I32, (qblk, nk), 0)
    kj = lax.broadcasted_iota(I32, (qblk, nk), 1)
    band_lo = (qi // CHUNK) * CHUNK + (2 * qblk - LEFT_REACH)
    allowed = (kj >= band_lo) & (kj < band_lo + LEFT_REACH + CHUNK) & (kj + (j - 2) * qblk >= 0)
    o_ref[...] = _attend(q_ref[...], [(k, v, lambda h: bias_ref[h], allowed)]).astype(o_ref.dtype)


def _attention_prompt(qa, ka, va, bias_tab, *, batch, seq):
    qblk = ATTN_QBLOCK
    assert seq % qblk == 0 and 2 * qblk >= LEFT_REACH
    nq = seq // qblk
    qspec = pl.BlockSpec((qblk, WIDTH), lambda b, j: (b * nq + j, 0))

    def kvspec(m):
        return pl.BlockSpec((qblk, WIDTH), lambda b, j: (b * nq + jnp.maximum(j - 2 + m, 0), 0))

    return pl.pallas_call(
        functools.partial(_attn_prompt_kernel, qblk=qblk),
        out_shape=jax.ShapeDtypeStruct(qa.shape, BF16),
        grid=(batch, nq),
        in_specs=[qspec, kvspec(0), kvspec(1), kvspec(2), kvspec(0), kvspec(1), kvspec(2),
                  _const_spec(bias_tab.shape)],
        out_specs=qspec,
        compiler_params=_cparams(("arbitrary", "arbitrary")),
        name="attention_prompt",
    )(qa, ka, ka, ka, va, va, va, bias_tab)


def _attn_sample_kernel(q_ref, kn_ref, vn_ref, kc_ref, vc_ref, bias_c_ref, bias_n_ref, o_ref):
    kc = kc_ref[0].astype(BF16)
    vc = vc_ref[0].astype(BF16)
    segments = [(kc, vc, lambda h: bias_c_ref[h], None),
                (kn_ref[...], vn_ref[...], lambda h: bias_n_ref[h], None)]
    o_ref[...] = _attend(q_ref[...], segments).astype(o_ref.dtype)


def _attention_sample(qa, ka, va, k_cache, v_cache, bias_c, bias_n, *, batch, seq):
    n_cache = k_cache.shape[1]
    tile = pl.BlockSpec((seq, WIDTH), lambda b: (b, 0))
    cache = pl.BlockSpec((1, n_cache, WIDTH), lambda b: (b, 0, 0))
    return pl.pallas_call(
        _attn_sample_kernel,
        out_shape=jax.ShapeDtypeStruct(qa.shape, BF16),
        grid=(batch,),
        in_specs=[tile, tile, tile, cache, cache, _const_spec(bias_c.shape), _const_spec(bias_n.shape)],
        out_specs=tile,
        compiler_params=_cparams(("arbitrary",)),
        name="attention_sample",
    )(qa, ka, va, k_cache, v_cache, bias_c, bias_n)


def _retention_kernel(*refs, has_init):
    if has_init:
        (q_ref, k_ref, v_ref, g_ref, dm_ref, xi_ref, zeta_ref, cdec_ref, gr_ref, s0_ref,
         ro_ref, sfin_ref, state) = refs
    else:
        (q_ref, k_ref, v_ref, g_ref, dm_ref, xi_ref, zeta_ref, cdec_ref, gr_ref,
         ro_ref, sfin_ref, state) = refs
    c = pl.program_id(1)

    @pl.when(c == 0)
    def _():
        state[...] = s0_ref[0] if has_init else jnp.zeros(state.shape, F32)

    lane = lax.broadcasted_iota(I32, (1, LANES), 1)
    srow = lax.broadcasted_iota(I32, (LANES, LANES), 0)
    scol = lax.broadcasted_iota(I32, (LANES, LANES), 1)
    same_head = (srow // HEAD_DIM) == (scol // HEAD_DIM)
    for g in range(GROUPS):
        sl = slice(g * LANES, (g + 1) * LANES)
        qg = q_ref[:, sl]
        kg = k_ref[:, sl]
        vg = v_ref[:, sl]
        s_prev = state[g]
        qx = (qg.astype(F32) * xi_ref[:, sl]).astype(BF16)
        o = jnp.dot(qx, s_prev.astype(BF16), preferred_element_type=F32)
        for par in range(PAIR):
            head_lanes = (lane // HEAD_DIM) == par
            qm = jnp.where(head_lanes, qg, jnp.zeros_like(qg))
            sc = lax.dot_general(qm, kg, (((1,), (1,)), ((), ())), preferred_element_type=F32)
            sc = sc * dm_ref[PAIR * g + par]
            oi = jnp.dot(sc.astype(BF16), vg, preferred_element_type=F32)
            o = o + jnp.where(head_lanes, oi, 0.0)
        kz = (kg.astype(F32) * zeta_ref[:, sl]).astype(BF16)
        kv = lax.dot_general(kz, vg, (((0,), (0,)), ((), ())), preferred_element_type=F32)
        state[g] = cdec_ref[:, sl] * s_prev + jnp.where(same_head, kv, 0.0)
        o2 = o * o
        rs = None
        for par in range(PAIR):
            head_lanes = (lane // HEAD_DIM) == par
            ms = jnp.sum(jnp.where(head_lanes, o2, 0.0), axis=-1, keepdims=True) * (1.0 / HEAD_DIM)
            r = lax.rsqrt(ms + EPS)
            rs = r if rs is None else jnp.where(head_lanes, r, rs)
        on = o * rs * gr_ref[:, sl]
        gate = g_ref[:, sl].astype(F32)
        ro_ref[:, sl] = (on * (gate * jax.nn.sigmoid(gate))).astype(ro_ref.dtype)

    @pl.when(c == pl.num_programs(1) - 1)
    def _():
        sfin_ref[0] = state[...]


def _retention(qb, kb, vb, gb, tabs, g_ret, s0_pairs, *, batch, seq, blk):
    assert seq % blk == 0
    nc = seq // blk
    dmask, xi, zeta, cdec = tabs
    tile = pl.BlockSpec((blk, WIDTH), lambda b, c: (b * nc + c, 0))
    st = pl.BlockSpec((1, GROUPS, LANES, LANES), lambda b, c: (b, 0, 0, 0))
    in_specs = [tile] * 4 + [_const_spec(dmask.shape), _const_spec(xi.shape), _const_spec(zeta.shape),
                             _const_spec(cdec.shape), _const_spec(g_ret.shape)]
    args = [qb, kb, vb, gb, dmask, xi, zeta, cdec, g_ret]
    if s0_pairs is not None:
        in_specs.append(st)
        args.append(s0_pairs)
    return pl.pallas_call(
        functools.partial(_retention_kernel, has_init=s0_pairs is not None),
        out_shape=[jax.ShapeDtypeStruct(qb.shape, BF16),
                   jax.ShapeDtypeStruct((batch, GROUPS, LANES, LANES), F32)],
        grid=(batch, nc),
        in_specs=in_specs,
        out_specs=[tile, st],
        scratch_shapes=[pltpu.VMEM((GROUPS, LANES, LANES), F32)],
        compiler_params=_cparams(("arbitrary", "arbitrary")),
        name="retention",
    )(*args)


def _post_mixer_kernel(h_ref, oa_ref, ro_ref, wo_ref, g_ref, wr_ref, br_ref, *rest, n_tiles):
    h1_ref, u_ref, idx_ref, gate_ref = rest[-4:]
    i = pl.program_id(0)

    @pl.when(i >= n_tiles)
    def _():
        u_ref[...] = jnp.zeros(u_ref.shape, F32)

    @pl.when(i < n_tiles)
    def _():
        _post_mixer_tile(h_ref, oa_ref, ro_ref, wo_ref, g_ref, wr_ref, br_ref, h1_ref, u_ref, idx_ref, gate_ref)


def _post_mixer_tile(h_ref, oa_ref, ro_ref, wo_ref, g_ref, wr_ref, br_ref, h1_ref, u_ref, idx_ref, gate_ref):
    h1 = (h_ref[...]
          + jnp.dot(oa_ref[...], wo_ref[:WIDTH, :], preferred_element_type=F32)
          + jnp.dot(ro_ref[...], wo_ref[WIDTH:, :], preferred_element_type=F32))
    h1_ref[...] = h1
    u = _rmsnorm(h1, g_ref[...])
    u_ref[...] = u
    logits = jnp.dot(u.astype(BF16), wr_ref[...], preferred_element_type=F32) + br_ref[...]
    lane = lax.broadcasted_iota(I32, logits.shape, 1)
    vals = jnp.where(lane < N_EXPERTS, logits, -jnp.inf)
    top_val, top_idx = [], []
    for _ in range(TOP_K):
        m = jnp.max(vals, axis=-1, keepdims=True)
        ix = jnp.min(jnp.where(vals == m, lane, LANES), axis=-1, keepdims=True)
        top_val.append(m)
        top_idx.append(ix)
        vals = jnp.where(lane == ix, -jnp.inf, vals)
    e = [jnp.exp(v - top_val[0]) for v in top_val]
    denom = functools.reduce(jnp.add, e)
    idx_out = jnp.full(logits.shape, -1, I32)
    gate_out = jnp.zeros(logits.shape, F32)
    for k in range(TOP_K):
        idx_out = jnp.where(lane == k, top_idx[k], idx_out)
        gate_out = jnp.where(lane == k, e[k] / denom, gate_out)
    idx_ref[...] = idx_out
    gate_ref[...] = gate_out


def _post_mixer(h2d, oa, ro, w_out_bf, g_ffn, w_router_pad, b_router_pad, *, tm, u_rows, u_row0, u_buf=None):
    t, d = h2d.shape
    assert u_row0 % tm == 0 and t % tm == 0
    n_tiles = t // tm
    n_fill = 0 if u_buf is not None else (u_rows - u_row0 - t) // tm
    assert u_buf is not None or u_rows == u_row0 + t + n_fill * tm
    clamp = lambda i: jnp.minimum(i, n_tiles - 1)
    row = pl.BlockSpec((tm, d), lambda i: (clamp(i), 0))
    half = pl.BlockSpec((tm, WIDTH), lambda i: (clamp(i), 0))
    lanes = pl.BlockSpec((tm, LANES), lambda i: (clamp(i), 0))
    in_specs = [row, half, half, _const_spec(w_out_bf.shape), _const_spec(g_ffn.shape),
                _const_spec(w_router_pad.shape), _const_spec(b_router_pad.shape)]
    args = [h2d, oa, ro, w_out_bf, g_ffn, w_router_pad, b_router_pad]
    aliases = {}
    if u_buf is not None:
        in_specs.append(pl.BlockSpec(memory_space=pl.ANY))
        args.append(u_buf)
        aliases = {len(args) - 1: 1}
    return pl.pallas_call(
        functools.partial(_post_mixer_kernel, n_tiles=n_tiles),
        out_shape=[jax.ShapeDtypeStruct((t, d), F32), jax.ShapeDtypeStruct((u_rows, d), F32),
                   jax.ShapeDtypeStruct((t, LANES), I32), jax.ShapeDtypeStruct((t, LANES), F32)],
        grid=(n_tiles + n_fill,),
        in_specs=in_specs,
        out_specs=[row, pl.BlockSpec((tm, d), lambda i: (i + u_row0 // tm, 0)), lanes, lanes],
        input_output_aliases=aliases,
        compiler_params=_cparams(("arbitrary",)),
        name="post_mixer",
    )(*args)


def _rank_kernel(idx_ref, rank_ref, count_ref, carry):
    i = pl.program_id(0)

    @pl.when(i == 0)
    def _():
        carry[...] = jnp.zeros(carry.shape, F32)

    idx = idx_ref[...]
    tm = idx.shape[0]
    lane = lax.broadcasted_iota(I32, idx.shape, 1)
    onehot = [lane == idx[:, k:k + 1] for k in range(TOP_K)]
    cnt = functools.reduce(jnp.add, [jnp.where(oh, 1.0, 0.0) for oh in onehot])
    r = lax.broadcasted_iota(I32, (tm, tm), 0)
    c = lax.broadcasted_iota(I32, (tm, tm), 1)
    earlier = jnp.where(c < r, 1.0, 0.0).astype(BF16)
    before = jnp.dot(earlier, cnt.astype(BF16), preferred_element_type=F32) + carry[...]
    rank = jnp.zeros(idx.shape, F32)
    for k in range(TOP_K):
        rk = jnp.sum(jnp.where(onehot[k], before, 0.0), axis=-1, keepdims=True)
        rank = jnp.where(lane == k, rk, rank)
    rank_ref[...] = rank.astype(I32)
    carry[...] = carry[...] + jnp.sum(cnt, axis=0, keepdims=True)
    count_ref[...] = carry[...].astype(I32)


def _rank_slots(idx_all, *, tm):
    t = idx_all.shape[0]
    lanes = pl.BlockSpec((tm, LANES), lambda i: (i, 0))
    return pl.pallas_call(
        _rank_kernel,
        out_shape=[jax.ShapeDtypeStruct((t, LANES), I32), jax.ShapeDtypeStruct((1, LANES), I32)],
        grid=(t // tm,),
        in_specs=[lanes],
        out_specs=[lanes, _const_spec((1, LANES))],
        scratch_shapes=[pltpu.VMEM((1, LANES), F32)],
        compiler_params=_cparams(("arbitrary",)),
        name="rank_slots",
    )(idx_all)


def _expert_kernel(blk_ref, exp_ref, lo_ref, hi_ref, first_ref, last_ref, newexp_ref,
                   src_ref, u_hbm, wgu_ref, bgu_ref, wd_ref, bd_ref, y_hbm,
                   xbuf, obuf, wgu_bf, wd_bf, gsem, ssem):
    i = pl.program_id(0)
    rows = xbuf.shape[0]

    def gather_copy(r):
        tok = lax.shift_right_logical(src_ref[0, 0, r], TOP_K.bit_length() - 1)
        return pltpu.make_async_copy(u_hbm.at[pl.ds(tok, 1)], xbuf.at[pl.ds(r, 1)], gsem)

    def scatter_copy(r):
        return pltpu.make_async_copy(obuf.at[pl.ds(r, 1)], y_hbm.at[pl.ds(src_ref[0, 0, r], 1)], ssem)

    def for_rows(fn):
        def body(r, carry):
            fn(r)
            return carry
        lax.fori_loop(0, rows, body, 0, unroll=8)

    @pl.when(first_ref[i] == 1)
    def _():
        for_rows(lambda r: gather_copy(r).start())
        obuf[...] = jnp.zeros(obuf.shape, F32)
        for_rows(lambda r: gather_copy(r).wait())

    @pl.when(newexp_ref[i] == 1)
    def _():
        wgu_bf[...] = wgu_ref[0].astype(BF16)
        wd_bf[...] = wd_ref[0].astype(BF16)

    lo = lo_ref[i]
    hi = hi_ref[i]

    @pl.when(hi > lo)
    def _():
        d_ff = wd_bf.shape[0]
        x = xbuf[...].astype(BF16)
        gu = jnp.dot(x, wgu_bf[...], preferred_element_type=F32) + bgu_ref[0]
        glu = jnp.minimum(gu[:, :d_ff], SWIGLU_LIMIT)
        lin = jnp.clip(gu[:, d_ff:], -SWIGLU_LIMIT, SWIGLU_LIMIT)
        act = glu * jax.nn.sigmoid(SWIGLU_ALPHA * glu) * (lin + 1.0)
        y = jnp.dot(act.astype(BF16), wd_bf[...], preferred_element_type=F32) + bd_ref[0]
        row = lax.broadcasted_iota(I32, (rows, 1), 0)
        obuf[...] = jnp.where((row >= lo) & (row < hi), y, obuf[...])

    @pl.when(last_ref[i] == 1)
    def _():
        for_rows(lambda r: scatter_copy(r).start())
        for_rows(lambda r: scatter_copy(r).wait())


def _expert_mlp(items, src3, u_all, w_gu, b_gu, w_down, b_down):
    n_blocks, _, rows = src3.shape
    n_rows = n_blocks * rows
    n_exp, d, d_gu = w_gu.shape
    d_ff = w_down.shape[1]
    n_items = items[0].shape[0]
    grid_spec = pltpu.PrefetchScalarGridSpec(
        num_scalar_prefetch=len(items),
        grid=(n_items,),
        in_specs=[
            pl.BlockSpec((1, 1, rows), lambda i, blk, *_: (blk[i], 0, 0), memory_space=pltpu.SMEM),
            pl.BlockSpec(memory_space=pl.ANY),
            pl.BlockSpec((1, d, d_gu), lambda i, blk, exp, *_: (exp[i], 0, 0)),
            pl.BlockSpec((1, 1, d_gu), lambda i, blk, exp, *_: (exp[i], 0, 0)),
            pl.BlockSpec((1, d_ff, d), lambda i, blk, exp, *_: (exp[i], 0, 0)),
            pl.BlockSpec((1, 1, d), lambda i, blk, exp, *_: (exp[i], 0, 0)),
        ],
        out_specs=pl.BlockSpec(memory_space=pl.ANY),
        scratch_shapes=[pltpu.VMEM((rows, d), F32), pltpu.VMEM((rows, d), F32),
                        pltpu.VMEM((d, d_gu), BF16), pltpu.VMEM((d_ff, d), BF16),
                        pltpu.SemaphoreType.DMA(()), pltpu.SemaphoreType.DMA(())],
    )
    return pl.pallas_call(
        _expert_kernel,
        out_shape=jax.ShapeDtypeStruct((n_rows, d), F32),
        grid_spec=grid_spec,
        compiler_params=_cparams(("arbitrary",)),
        name="expert_mlp",
    )(*items, src3, u_all, w_gu, b_gu.reshape(n_exp, 1, d_gu), w_down, b_down.reshape(n_exp, 1, d))


def _plan_dispatch(idx, rank, counts, *, rows):
    n_rows = idx.shape[0] * TOP_K
    n_blocks = n_rows // rows
    n_exp = counts.shape[0]
    ends = jnp.cumsum(counts)
    starts = ends - counts
    dest = starts[idx] + rank
    src = jnp.zeros((n_rows,), I32).at[dest.reshape(-1)].set(jnp.arange(n_rows, dtype=I32))
    block_cuts = jnp.arange(1, n_blocks, dtype=I32) * rows
    expert_cuts = ends[:-1]
    block_pos = jnp.arange(n_blocks - 1, dtype=I32) + jnp.sum(expert_cuts[None, :] < block_cuts[:, None], axis=1)
    expert_pos = jnp.arange(n_exp - 1, dtype=I32) + jnp.minimum(expert_cuts // rows, n_blocks - 1)
    cuts = (jnp.zeros((n_blocks + n_exp - 2,), I32).at[block_pos].set(block_cuts).at[expert_pos].set(expert_cuts))
    lo_abs = jnp.concatenate([jnp.zeros((1,), I32), cuts])
    hi_abs = jnp.concatenate([cuts, jnp.full((1,), n_rows, I32)])
    blk = jnp.minimum(lo_abs // rows, n_blocks - 1)
    exp = jnp.minimum(jnp.sum(ends[None, :] <= lo_abs[:, None], axis=1).astype(I32), n_exp - 1)
    lo = lo_abs - blk * rows
    hi = hi_abs - blk * rows
    changed = lambda a: jnp.concatenate([jnp.ones((1,), I32), (a[1:] != a[:-1]).astype(I32)])
    first = changed(blk)
    last = jnp.concatenate([first[1:], jnp.ones((1,), I32)])
    items = (blk, exp, lo, hi, first, last, changed(exp))
    return items, src.reshape(n_blocks, 1, rows)


def _final_kernel(h1_ref, ye_ref, gate_ref, p_ref, wpp_ref, wpg_ref, gple_ref, gfin_ref, y_ref):
    d = h1_ref.shape[1]
    gates = gate_ref[...]
    h = h1_ref[...]
    for k in range(TOP_K):
        h = h + ye_ref[:, k * d:(k + 1) * d] * gates[:, k:k + 1]
    gate = jax.nn.sigmoid(jnp.dot(_rmsnorm(h, gple_ref[...]).astype(BF16), wpg_ref[...],
                                  preferred_element_type=F32))
    h = h + jnp.dot(p_ref[...].astype(BF16), wpp_ref[...], preferred_element_type=F32) * gate
    y_ref[...] = _rmsnorm(h, gfin_ref[...])


def _finish(h1, ye, gates, p2d, w_ple_proj_bf, w_ple_gate_bf, g_ple, g_final, *, tm, ye_row0):
    t, d = h1.shape
    assert ye_row0 % tm == 0
    row = pl.BlockSpec((tm, d), lambda i: (i, 0))
    return pl.pallas_call(
        _final_kernel,
        out_shape=jax.ShapeDtypeStruct((t, d), F32),
        grid=(t // tm,),
        in_specs=[row, pl.BlockSpec((tm, TOP_K * d), lambda i: (i + ye_row0 // tm, 0)),
                  pl.BlockSpec((tm, LANES), lambda i: (i, 0)),
                  pl.BlockSpec((tm, p2d.shape[1]), lambda i: (i, 0)), _const_spec(w_ple_proj_bf.shape),
                  _const_spec(w_ple_gate_bf.shape), _const_spec(g_ple.shape), _const_spec(g_final.shape)],
        out_specs=row,
        compiler_params=_cparams(("arbitrary",)),
        name="finish",
    )(h1, ye, gates, p2d, w_ple_proj_bf, w_ple_gate_bf, g_ple, g_final)


def _rotary_tables(pos):
    half = HEAD_DIM // 2
    freq = ROPE_BASE ** (-jnp.linspace(0.0, 1.0, half, dtype=F32))
    ang = pos.astype(F32)[:, None] * freq[None, :]
    cos = jnp.tile(jnp.cos(ang), (1, 2 * HEADS))
    sin = jnp.tile(jnp.concatenate([-jnp.sin(ang), jnp.sin(ang)], axis=1), (1, HEADS))
    return cos, sin


def _retention_tables(blk):
    log_g = jnp.log1p(-jnp.exp2(-5.0 - jnp.arange(HEADS, dtype=F32)))
    idx = jnp.arange(blk, dtype=F32)
    rel = idx[:, None] - idx[None, :]
    dmask = jnp.where(rel >= 0, jnp.exp(log_g[:, None, None] * jnp.maximum(rel, 0.0)), 0.0)
    per_lane = lambda a: jnp.repeat(a, HEAD_DIM, axis=0).T
    xi = per_lane(jnp.exp(log_g[:, None] * (idx + 1.0)[None, :]))
    zeta = per_lane(jnp.exp(log_g[:, None] * (blk - 1.0 - idx)[None, :]))
    cdec = per_lane(jnp.exp(log_g * blk)[:, None])
    return dmask, xi, zeta, cdec


def _rel_bias_table(rel_bias, q_pos, k_pos):
    dist = jnp.clip(q_pos[:, None] - k_pos[None, :], -MAX_REL, MAX_REL) + MAX_REL
    return rel_bias[:, dist].astype(F32)


def _pair_states(s):
    b = s.shape[0]
    s = s.reshape(b, GROUPS, PAIR, HEAD_DIM, HEAD_DIM)
    eye = jnp.eye(PAIR, dtype=s.dtype)
    return jnp.einsum("bgpkv,pq->bgpkqv", s, eye).reshape(b, GROUPS, LANES, LANES)


def _unpair_states(sp):
    b = sp.shape[0]
    sp = sp.reshape(b, GROUPS, PAIR, HEAD_DIM, PAIR, HEAD_DIM)
    return jnp.stack([sp[:, :, p, :, p, :] for p in range(PAIR)], axis=2).reshape(b, HEADS, HEAD_DIM, HEAD_DIM)


def _mixer_and_router(x, pos, attn_fn, s0, tm, ret_blk, w, u_rows, u_row0, u_buf):
    b, s, d = x.shape
    keep = min(LEFT_REACH, s)
    x2d = x.reshape(b * s, d)
    cos_t, sin_t = _rotary_tables(pos)
    if tm > s:
        cos_t, sin_t = jnp.tile(cos_t, (tm // s, 1)), jnp.tile(sin_t, (tm // s, 1))
    qa, ka, va, qb, kb, vb, gb, ka_keep, va_keep = _in_project(
        x2d, w["g_mix"], w["w_in"], cos_t, sin_t, tm=tm, seq=s, keep=keep)
    oa = attn_fn(qa, ka, va)
    ro, s_fin = _retention(qb, kb, vb, gb, _retention_tables(ret_blk), w["g_ret"],
                           None if s0 is None else _pair_states(s0), batch=b, seq=s, blk=ret_blk)
    h1, u, idx, gates = _post_mixer(x2d, oa, ro, w["w_out"], w["g_ffn"], w["w_router"], w["b_router"], tm=tm,
                                    u_rows=u_rows, u_row0=u_row0, u_buf=u_buf)
    k_rows = ka_keep.reshape(1, b, keep, HEADS, HEAD_DIM)
    v_rows = va_keep.reshape(1, b, keep, HEADS, HEAD_DIM)
    return h1, u, idx, gates, k_rows, v_rows, _unpair_states(s_fin)[None]


def kernel(x_prompt, x_sample, cache_k_attn, cache_v_attn, state_ret, p_prompt, p_sample, g_mix, w_in, rel_bias,
           g_ret, w_out, g_ffn, w_router, b_router, w_gu, b_gu, w_down, b_down, g_ple, w_ple_gate, w_ple_proj,
           g_final):
    assert g_mix.shape[0] == 1, "single layer"
    bp, sp, d = x_prompt.shape
    bs, ss, _ = x_sample.shape
    past = cache_k_attn.shape[2]
    tp, ts = bp * sp, bs * ss
    n_exp = w_router.shape[-1]
    w = {
        "g_mix": g_mix, "g_ret": g_ret, "g_ffn": g_ffn,
        "w_in": w_in[0].astype(BF16), "w_out": w_out[0].astype(BF16),
        "w_router": jnp.pad(w_router[0], ((0, 0), (0, LANES - n_exp))).astype(BF16),
        "b_router": jnp.pad(b_router, ((0, 0), (0, LANES - n_exp))),
    }

    bias_p = _rel_bias_table(rel_bias[0], 2 * ATTN_QBLOCK + jnp.arange(ATTN_QBLOCK), jnp.arange(3 * ATTN_QBLOCK))
    attn_p = functools.partial(_attention_prompt, bias_tab=bias_p, batch=bp, seq=sp)
    t_all = tp + ts
    t_pad = t_all + (-t_all) % TOKEN_TILE
    h1_p, u_all, idx_p, gates_p, k_p, v_p, s_p = _mixer_and_router(
        x_prompt, jnp.arange(sp), attn_p, None, TOKEN_TILE, RET_BLOCK, w, t_pad, 0, None)

    pos_s = PAST_LEN + jnp.arange(ss)
    k_pos_cache = pos_s[0] - past + jnp.arange(past)
    attn_s = functools.partial(
        _attention_sample, k_cache=cache_k_attn[0].reshape(bs, past, WIDTH),
        v_cache=cache_v_attn[0].reshape(bs, past, WIDTH),
        bias_c=_rel_bias_table(rel_bias[0], pos_s, k_pos_cache), bias_n=_rel_bias_table(rel_bias[0], pos_s, pos_s),
        batch=bs, seq=ss)
    h1_s, u_all, idx_s, gates_s, k_s, v_s, s_s = _mixer_and_router(
        x_sample, pos_s, attn_s, state_ret[0], ts, min(CHUNK, ss), w, t_pad, tp, u_all)

    assert (t_all * TOP_K) % EXPERT_ROWS == 0
    idx_all = jnp.concatenate([idx_p, idx_s, jnp.full((t_pad - t_all, LANES), -1, I32)], axis=0)
    rank_all, counts = _rank_slots(idx_all, tm=TOKEN_TILE)
    items, src3 = _plan_dispatch(idx_all[:t_all, :TOP_K], rank_all[:t_all, :TOP_K], counts[0, :n_exp],
                                 rows=EXPERT_ROWS)
    ye = _expert_mlp(items, src3, u_all, w_gu[0], b_gu[0], w_down[0], b_down[0]).reshape(t_all, TOP_K * d)

    w_pp, w_pg = w_ple_proj[0].astype(BF16), w_ple_gate[0].astype(BF16)
    g_fin = g_final.reshape(1, d)
    y_p = _finish(h1_p, ye, gates_p, p_prompt[0].reshape(tp, -1), w_pp, w_pg, g_ple, g_fin, tm=256, ye_row0=0)
    y_s = _finish(h1_s, ye, gates_s, p_sample[0].reshape(ts, -1), w_pp, w_pg, g_ple, g_fin, tm=ts, ye_row0=tp)
    return (y_p.reshape(bp, sp, d), y_s.reshape(bs, ss, d), k_p, v_p, s_p, k_s, v_s, s_s)
```
